```python
import jax, jax.numpy as jnp
from jax import lax
import numpy as np

D_MODEL = 1024
BATCH = 2
SEQ = 8192
DEPTH = 1
DEC_BATCH = 32
DEC_SEQ = 8
PAST_LEN = 16384
PAGE_SIZE = 128

FOX_HEADS = 8
FOX_HEAD_DIM = 64
GLA_HEADS = 4
GLA_KEY_DIM = 64
GLA_VAL_DIM = 128
GLA_GATE_RANK = 16
GLA_GATE_TAU = 16.0
GLA_CHUNK = 64
Q_BLOCK = 128
FFN_DIM = 2752
NORM_EPS = 1e-6
PROJ_SIZES = (
    FOX_HEADS * FOX_HEAD_DIM,
    FOX_HEADS * FOX_HEAD_DIM,
    FOX_HEADS * FOX_HEAD_DIM,
    FOX_HEADS,
    GLA_HEADS * GLA_KEY_DIM,
    GLA_HEADS * GLA_KEY_DIM,
    GLA_HEADS * GLA_VAL_DIM,
    GLA_HEADS * GLA_VAL_DIM,
    GLA_GATE_RANK,
)
PROJ_WIDTH = sum(PROJ_SIZES)
MIX_WIDTH = FOX_HEADS * FOX_HEAD_DIM + GLA_HEADS * GLA_VAL_DIM

kernel_name = 'fox_gla_parallel_heads_macaron_step'


def rmsnorm(x, g):
    x32 = x.astype(jnp.float32)
    y = x32 * lax.rsqrt(jnp.mean(x32 * x32, axis=-1, keepdims=True) + NORM_EPS)
    return (y * g.astype(jnp.float32)).astype(x.dtype)


def swiglu(x, w_gu, w_down):
    gate, up = jnp.split(x @ w_gu, 2, axis=-1)
    return (jax.nn.silu(gate) * up) @ w_down


def macaron_half(x, g_pre, w_gu, w_down, g_post):
    return x + 0.5 * rmsnorm(swiglu(rmsnorm(x, g_pre), w_gu, w_down), g_post)


def project(u, w_in, b_f, w_a2, b_a):
    B, T, _ = u.shape
    z = u @ w_in
    offs = np.cumsum(PROJ_SIZES)[:-1].tolist()
    fq, fk, fv, ff, gq, gk, gv, gg, ga = jnp.split(z, offs, axis=-1)
    fq = fq.reshape(B, T, FOX_HEADS, FOX_HEAD_DIM)
    fk = fk.reshape(B, T, FOX_HEADS, FOX_HEAD_DIM)
    fv = fv.reshape(B, T, FOX_HEADS, FOX_HEAD_DIM)
    logf = jax.nn.log_sigmoid((ff + b_f).astype(jnp.float32))
    gq = gq.astype(jnp.float32).reshape(B, T, GLA_HEADS, GLA_KEY_DIM) * (GLA_KEY_DIM ** -0.5)
    gk = gk.astype(jnp.float32).reshape(B, T, GLA_HEADS, GLA_KEY_DIM)
    gv = gv.astype(jnp.float32).reshape(B, T, GLA_HEADS, GLA_VAL_DIM)
    gg = gg.reshape(B, T, GLA_HEADS, GLA_VAL_DIM)
    log_a = jax.nn.log_sigmoid((ga @ w_a2 + b_a).astype(jnp.float32)) / GLA_GATE_TAU
    log_a = log_a.reshape(B, T, GLA_HEADS, GLA_KEY_DIM)
    return fq, fk, fv, logf, gq, gk, gv, gg, log_a


def merge(fox_o, gla_o, gla_g, g_gla, w_o):
    B, T = fox_o.shape[:2]
    gla_o = rmsnorm(gla_o, g_gla).astype(fox_o.dtype) * jax.nn.silu(gla_g)
    mix = jnp.concatenate([fox_o.reshape(B, T, -1), gla_o.reshape(B, T, -1)], axis=-1)
    return mix @ w_o


def fox_prompt(q, k, v, logf):
    B, T, H, D = q.shape
    c = jnp.swapaxes(jnp.cumsum(logf, axis=1), 1, 2)
    k_pos = jnp.arange(T)
    scale = D ** -0.5

    def block(i):
        start = i * Q_BLOCK
        qb = lax.dynamic_slice_in_dim(q, start, Q_BLOCK, axis=1)
        cb = lax.dynamic_slice_in_dim(c, start, Q_BLOCK, axis=2)
        s = (jnp.einsum('bqhd,bkhd->bhqk', qb, k).astype(jnp.float32) * scale
             + cb[..., :, None] - c[..., None, :])
        mask = (start + jnp.arange(Q_BLOCK))[:, None] >= k_pos[None, :]
        p = jax.nn.softmax(jnp.where(mask, s, -jnp.inf), axis=-1).astype(v.dtype)
        return jnp.einsum('bhqk,bkhd->bqhd', p, v)

    o = lax.map(block, jnp.arange(T // Q_BLOCK))
    return jnp.moveaxis(o, 0, 1).reshape(B, T, H, D)


def fox_sample(q, k, v, logf, pool_k, pool_v, pool_logf, page_table):
    DB, S, H, D = q.shape
    P = page_table.shape[1] * PAGE_SIZE
    pk = pool_k[page_table].reshape(DB, P, H, D)
    pv = pool_v[page_table].reshape(DB, P, H, D)
    plf = pool_logf[page_table].reshape(DB, P, H).astype(jnp.float32)
    rev = lax.cumsum(plf, axis=1, reverse=True)
    after = jnp.concatenate([rev[:, 1:], jnp.zeros_like(rev[:, :1])], axis=1)
    cn = jnp.swapaxes(jnp.cumsum(logf, axis=1), 1, 2)
    scale = D ** -0.5
    s_past = (jnp.einsum('bqhd,bkhd->bhqk', q, pk).astype(jnp.float32) * scale
              + cn[..., :, None] + jnp.swapaxes(after, 1, 2)[:, :, None, :])
    s_new = (jnp.einsum('bqhd,bkhd->bhqk', q, k).astype(jnp.float32) * scale
             + cn[..., :, None] - cn[..., None, :])
    causal = jnp.tril(jnp.ones((S, S), dtype=bool))
    s_new = jnp.where(causal, s_new, -jnp.inf)
    p = jax.nn.softmax(jnp.concatenate([s_past, s_new], axis=-1), axis=-1).astype(v.dtype)
    return (jnp.einsum('bhqk,bkhd->bqhd', p[..., :P], pv)
            + jnp.einsum('bhqk,bkhd->bqhd', p[..., P:], v))


def gla_chunk(q, k, v, log_a, s0):
    L = q.shape[1]
    b = jnp.cumsum(log_a, axis=1)
    o_inter = jnp.einsum('blhk,bhkv->blhv', q * jnp.exp(b), s0)
    causal = jnp.tril(jnp.ones((L, L), dtype=bool))
    diff = b[:, :, None] - b[:, None, :]
    decay = jnp.exp(jnp.where(causal[None, :, :, None, None], diff, -jnp.inf))
    attn = jnp.einsum('bthk,bshk,btshk->bhts', q, k, decay)
    o_intra = jnp.einsum('bhts,bshv->bthv', attn, v)
    b_last = b[:, -1]
    k_dec = k * jnp.exp(b_last[:, None] - b)
    s_new = jnp.exp(b_last)[..., None] * s0 + jnp.einsum('blhk,blhv->bhkv', k_dec, v)
    return o_inter + o_intra, s_new


def gla_prompt(q, k, v, log_a):
    B, T, H, K = q.shape
    nc = T // GLA_CHUNK

    def to_chunks(a):
        return jnp.moveaxis(a.reshape((B, nc, GLA_CHUNK) + a.shape[2:]), 1, 0)

    def step(s, xs):
        o, s_new = gla_chunk(xs[0], xs[1], xs[2], xs[3], s)
        return s_new, o

    s0 = jnp.zeros((B, H, K, GLA_VAL_DIM), jnp.float32)
    s_fin, o = lax.scan(step, s0, (to_chunks(q), to_chunks(k), to_chunks(v), to_chunks(log_a)))
    return jnp.moveaxis(o, 0, 1).reshape(B, T, H, GLA_VAL_DIM), s_fin


def setup_inputs(seed: int = 0) -> dict:
    key = jax.random.key(seed)
    ks = jax.random.split(key, 32)
    n_pages = PAST_LEN // PAGE_SIZE
    used = DEC_BATCH * n_pages
    n_pool = used + max(1, used // 4)

    def w(k, shape, fan_in):
        return jax.random.normal(k, (DEPTH,) + shape, jnp.float32) * (fan_in ** -0.5)

    def gain(k, n):
        return 1.0 + 0.05 * jax.random.normal(k, (DEPTH, n), jnp.float32)

    x_prompt = jax.random.normal(ks[0], (BATCH, SEQ, D_MODEL), jnp.float32)
    x_sample = jax.random.normal(ks[1], (DEC_BATCH, DEC_SEQ, D_MODEL), jnp.float32)
    cache_k = jax.random.normal(ks[2], (DEPTH, n_pool, PAGE_SIZE, FOX_HEADS, FOX_HEAD_DIM), jnp.float32)
    cache_v = jax.random.normal(ks[3], (DEPTH, n_pool, PAGE_SIZE, FOX_HEADS, FOX_HEAD_DIM), jnp.float32)
    logf_logit = float(np.log(PAST_LEN)) + 2.0 + 0.5 * jax.random.normal(
        ks[4], (DEPTH, n_pool, PAGE_SIZE, FOX_HEADS), jnp.float32)
    cache_logf = jax.nn.log_sigmoid(logf_logit)
    state_gla = 0.5 * jax.random.normal(ks[5], (DEPTH, DEC_BATCH, GLA_HEADS, GLA_KEY_DIM, GLA_VAL_DIM), jnp.float32)
    page_table = jax.random.permutation(ks[6], n_pool)[:used].reshape(DEC_BATCH, n_pages).astype(jnp.int32)
    return {
        'x_prompt': x_prompt,
        'x_sample': x_sample,
        'cache_k': cache_k,
        'cache_v': cache_v,
        'cache_logf': cache_logf,
        'state_gla': state_gla,
        'page_table': page_table,
        'ffn1_norm_pre': gain(ks[7], D_MODEL),
        'ffn1_w_gu': w(ks[8], (D_MODEL, 2 * FFN_DIM), D_MODEL),
        'ffn1_w_down': w(ks[9], (FFN_DIM, D_MODEL), FFN_DIM),
        'ffn1_norm_post': gain(ks[10], D_MODEL),
        'mix_norm_pre': gain(ks[11], D_MODEL),
        'w_in': w(ks[12], (D_MODEL, PROJ_WIDTH), D_MODEL),
        'b_forget': jax.random.uniform(ks[13], (DEPTH, FOX_HEADS), jnp.float32, 1.0, 5.0),
        'w_gate_up': w(ks[14], (GLA_GATE_RANK, GLA_HEADS * GLA_KEY_DIM), GLA_GATE_RANK),
        'b_gate': 0.1 * jax.random.normal(ks[15], (DEPTH, GLA_HEADS * GLA_KEY_DIM), jnp.float32),
        'gla_norm': gain(ks[16], GLA_VAL_DIM),
        'w_out': w(ks[17], (MIX_WIDTH, D_MODEL), MIX_WIDTH),
        'mix_norm_post': gain(ks[18], D_MODEL),
        'ffn2_norm_pre': gain(ks[19], D_MODEL),
        'ffn2_w_gu': w(ks[20], (D_MODEL, 2 * FFN_DIM), D_MODEL),
        'ffn2_w_down': w(ks[21], (FFN_DIM, D_MODEL), FFN_DIM),
        'ffn2_norm_post': gain(ks[22], D_MODEL),
    }


def reference(x_prompt, x_sample, cache_k, cache_v, cache_logf, state_gla, page_table,
              ffn1_norm_pre, ffn1_w_gu, ffn1_w_down, ffn1_norm_post,
              mix_norm_pre, w_in, b_forget, w_gate_up, b_gate, gla_norm, w_out, mix_norm_post,
              ffn2_norm_pre, ffn2_w_gu, ffn2_w_down, ffn2_norm_post):
    hp = x_prompt
    hs = x_sample
    kp_l, vp_l, fp_l, sp_l = [], [], [], []
    ks_l, vs_l, fs_l, ss_l = [], [], [], []
    for l in range(DEPTH):
        hp = macaron_half(hp, ffn1_norm_pre[l], ffn1_w_gu[l], ffn1_w_down[l], ffn1_norm_post[l])
        fq, fk, fv, lf, gq, gk, gv, gg, la = project(rmsnorm(hp, mix_norm_pre[l]), w_in[l],
                                                     b_forget[l], w_gate_up[l], b_gate[l])
        fo = fox_prompt(fq, fk, fv, lf)
        go, s_p = gla_prompt(gq, gk, gv, la)
        hp = hp + rmsnorm(merge(fo, go, gg, gla_norm[l], w_out[l]), mix_norm_post[l])
        hp = macaron_half(hp, ffn2_norm_pre[l], ffn2_w_gu[l], ffn2_w_down[l], ffn2_norm_post[l])
        kp_l.append(fk.astype(cache_k.dtype))
        vp_l.append(fv.astype(cache_v.dtype))
        fp_l.append(lf.astype(cache_logf.dtype))
        sp_l.append(s_p.astype(state_gla.dtype))
        hs = macaron_half(hs, ffn1_norm_pre[l], ffn1_w_gu[l], ffn1_w_down[l], ffn1_norm_post[l])
        fq, fk, fv, lf, gq, gk, gv, gg, la = project(rmsnorm(hs, mix_norm_pre[l]), w_in[l],
                                                     b_forget[l], w_gate_up[l], b_gate[l])
        fo = fox_sample(fq, fk, fv, lf, cache_k[l], cache_v[l], cache_logf[l], page_table)
        go, s_s = gla_chunk(gq, gk, gv, la, state_gla[l].astype(jnp.float32))
        hs = hs + rmsnorm(merge(fo, go, gg, gla_norm[l], w_out[l]), mix_norm_post[l])
        hs = macaron_half(hs, ffn2_norm_pre[l], ffn2_w_gu[l], ffn2_w_down[l], ffn2_norm_post[l])
        ks_l.append(fk.astype(cache_k.dtype))
        vs_l.append(fv.astype(cache_v.dtype))
        fs_l.append(lf.astype(cache_logf.dtype))
        ss_l.append(s_s.astype(state_gla.dtype))
    new_k_prompt = jnp.stack(kp_l, 0)
    new_v_prompt = jnp.stack(vp_l, 0)
    new_logf_prompt = jnp.stack(fp_l, 0)
    new_gla_prompt = jnp.stack(sp_l, 0)
    new_k_sample = jnp.stack(ks_l, 0)
    new_v_sample = jnp.stack(vs_l, 0)
    new_logf_sample = jnp.stack(fs_l, 0)
    new_gla_sample = jnp.stack(ss_l, 0)
    return (hp, hs, new_k_prompt, new_v_prompt, new_logf_prompt, new_gla_prompt,
            new_k_sample, new_v_sample, new_logf_sample, new_gla_sample)
```

```python
import functools

import numpy as np
import jax
import jax.numpy as jnp
from jax import lax
from jax.experimental import pallas as pl
from jax.experimental.pallas import tpu as pltpu

BF16 = jnp.bfloat16
F32 = jnp.float32

D_MODEL = 1024
FOX_HEADS = 8
FOX_DIM = 64
FOX_WIDTH = FOX_HEADS * FOX_DIM
GLA_HEADS = 4
GLA_KEY = 64
GLA_VAL = 128
GLA_KW = GLA_HEADS * GLA_KEY
GLA_VW = GLA_HEADS * GLA_VAL
GATE_RANK = 16
GATE_TAU = 16.0
GLA_CHUNK = 64
FFN_DIM = 2752
LANES = 128
FFN_PAD = 2816
FFN_CHUNK = 256
NORM_EPS = 1e-6
PAGE = 128
LOG2E = 1.4426950408889634
NEG = -1e30
VMEM_LIMIT_BYTES = 56 * 1024 * 1024
DEC_PAGES_PER_STEP = 8


def _dot(a, b):
    return jnp.dot(a, b, preferred_element_type=F32)


def _dot_nt(a, b):
    return lax.dot_general(a, b, (((1,), (1,)), ((), ())), preferred_element_type=F32)


def _dot_tn(a, b):
    return lax.dot_general(a, b, (((0,), (0,)), ((), ())), preferred_element_type=F32)


def _split3(x):
    hi = x.astype(BF16)
    r = x - hi.astype(F32)
    mid = r.astype(BF16)
    r = r - mid.astype(F32)
    return hi, mid, r.astype(BF16)


def _rms(x, g):
    ms = jnp.mean(x * x, axis=-1, keepdims=True)
    return x * lax.rsqrt(ms + NORM_EPS) * g


def _silu(x):
    return x * jax.nn.sigmoid(x)


def _log_sigmoid(x):
    return jnp.minimum(x, 0.0) - jnp.log1p(jnp.exp(-jnp.abs(x)))


def _params(sem):
    return pltpu.CompilerParams(dimension_semantics=sem, vmem_limit_bytes=VMEM_LIMIT_BYTES)


def _full(shape):
    nd = len(shape)
    return pl.BlockSpec(shape, lambda *_: (0,) * nd)


def _ffn_body(x, gpre, wgu_ref, wd_ref, gpost):
    xb = _rms(x, gpre).astype(BF16)
    acc = jnp.zeros((x.shape[0], D_MODEL), F32)
    for c in range(FFN_PAD // FFN_CHUNK):
        lo, hi = c * FFN_CHUNK, (c + 1) * FFN_CHUNK
        gate = _dot(xb, wgu_ref[:, lo:hi])
        up = _dot(xb, wgu_ref[:, FFN_PAD + lo:FFN_PAD + hi])
        act = (_silu(gate) * up).astype(BF16)
        acc = acc + _dot(act, wd_ref[lo:hi, :])
    return x + 0.5 * _rms(acc, gpost)


def _ffn_kernel(x_ref, gpre_ref, wgu_ref, wd_ref, gpost_ref, o_ref):
    o_ref[...] = _ffn_body(x_ref[...], gpre_ref[...], wgu_ref, wd_ref, gpost_ref[...])


def _merge_ffn_kernel(fo_ref, go_ref, h_ref, wo_ref, gmix_ref, gpre_ref, wgu_ref, wd_ref, gpost_ref,
                      o_ref):
    mix = _dot(fo_ref[...], wo_ref[0:FOX_WIDTH, :]) + _dot(go_ref[...], wo_ref[FOX_WIDTH:, :])
    h2 = h_ref[...] + _rms(mix, gmix_ref[...])
    o_ref[...] = _ffn_body(h2, gpre_ref[...], wgu_ref, wd_ref, gpost_ref[...])


def _ffn_call(x, gpre, wgu, wd, gpost, tm):
    n = x.shape[0]
    row = pl.BlockSpec((tm, D_MODEL), lambda i: (i, 0))
    return pl.pallas_call(
        _ffn_kernel,
        grid=(n // tm,),
        in_specs=[row, _full(gpre.shape), _full(wgu.shape), _full(wd.shape), _full(gpost.shape)],
        out_specs=row,
        out_shape=jax.ShapeDtypeStruct((n, D_MODEL), F32),
        compiler_params=_params(("arbitrary",)),
        name="ffn",
    )(x, gpre, wgu, wd, gpost)


def _merge_ffn_call(fo, go, h, wo, gmix, gpre, wgu, wd, gpost, tm):
    n = h.shape[0]
    row = pl.BlockSpec((tm, D_MODEL), lambda i: (i, 0))
    half = pl.BlockSpec((tm, FOX_WIDTH), lambda i: (i, 0))
    return pl.pallas_call(
        _merge_ffn_kernel,
        grid=(n // tm,),
        in_specs=[half, half, row, _full(wo.shape), _full(gmix.shape), _full(gpre.shape),
                  _full(wgu.shape), _full(wd.shape), _full(gpost.shape)],
        out_specs=row,
        out_shape=jax.ShapeDtypeStruct((n, D_MODEL), F32),
        compiler_params=_params(("arbitrary",)),
        name="merge_ffn",
    )(fo, go, h, wo, gmix, gpre, wgu, wd, gpost)


_Q0, _K0, _V0 = 0, FOX_WIDTH, 2 * FOX_WIDTH
_GQ0 = 3 * FOX_WIDTH
_GK0 = _GQ0 + GLA_KW
_GV0 = _GK0 + GLA_KW
_GG0 = _GV0 + GLA_VW
_MAIN_W = _GG0 + GLA_VW


def _proj_kernel(steps_per_seq, h_ref, g_ref, wm_ref, wft_ref, wga_ref, wa2_ref, bf_ref, ba_ref,
                 u_ref, qb_ref, kb_ref, vb_ref, fk_ref, fv_ref, lft_ref, ct_ref, gq_ref, gk_ref,
                 la_ref, gv_ref, gg_ref, carry_ref):
    i = pl.program_id(0)
    ub = _rms(h_ref[...], g_ref[...]).astype(BF16)
    z = _dot(ub, wm_ref[...])
    qb_ref[...] = z[:, _Q0:_K0].astype(BF16)
    k = z[:, _K0:_V0]
    fk_ref[...] = k
    kb_ref[...] = k.astype(BF16)
    v = z[:, _V0:_GQ0]
    fv_ref[...] = v
    vb_ref[...] = v.astype(BF16)
    gq_ref[...] = z[:, _GQ0:_GK0]
    gk_ref[...] = z[:, _GK0:_GV0]
    gv_ref[...] = z[:, _GV0:_GG0]
    gg_ref[...] = z[:, _GG0:_MAIN_W]

    lf = _log_sigmoid(_dot_nt(wft_ref[...], ub) + bf_ref[...])
    lft_ref[...] = lf
    hi, mid, lo = _split3(lf)
    u = u_ref[...]
    loc = _dot(hi, u) + _dot(mid, u) + _dot(lo, u)

    @pl.when(i % steps_per_seq == 0)
    def _():
        carry_ref[...] = jnp.zeros_like(carry_ref)

    c = loc + carry_ref[:, 0:1]
    tm = c.shape[1]
    carry_ref[...] = jnp.broadcast_to(c[:, tm - 1:tm], carry_ref.shape)
    ct_ref[...] = c * LOG2E

    ga = _dot(ub, wga_ref[...]).astype(BF16)
    pre = _dot(ga, wa2_ref[...]) + ba_ref[...]
    la_ref[...] = _log_sigmoid(pre) * (1.0 / GATE_TAU)


def _proj_call(h, g, wm, wft, wga, wa2, bf, ba, u, nb, seq, tm):
    n = h.shape[0]
    spt = seq // tm
    row = lambda w: pl.BlockSpec((tm, w), lambda i: (i, 0))
    tok = pl.BlockSpec((None, FOX_HEADS, tm), lambda i: (i // spt, 0, i % spt))
    sds = jax.ShapeDtypeStruct
    outs = [
        (sds((n, FOX_WIDTH), BF16), row(FOX_WIDTH)),
        (sds((n, FOX_WIDTH), BF16), row(FOX_WIDTH)),
        (sds((n, FOX_WIDTH), BF16), row(FOX_WIDTH)),
        (sds((n, FOX_WIDTH), F32), row(FOX_WIDTH)),
        (sds((n, FOX_WIDTH), F32), row(FOX_WIDTH)),
        (sds((nb, FOX_HEADS, seq), F32), tok),
        (sds((nb, FOX_HEADS, seq), F32), tok),
        (sds((n, GLA_KW), F32), row(GLA_KW)),
        (sds((n, GLA_KW), F32), row(GLA_KW)),
        (sds((n, GLA_KW), F32), row(GLA_KW)),
        (sds((n, GLA_VW), F32), row(GLA_VW)),
        (sds((n, GLA_VW), F32), row(GLA_VW)),
    ]
    return pl.pallas_call(
        functools.partial(_proj_kernel, spt),
        grid=(n // tm,),
        in_specs=[row(D_MODEL), _full(g.shape), _full(wm.shape), _full(wft.shape), _full(wga.shape),
                  _full(wa2.shape), _full(bf.shape), _full(ba.shape), _full(u.shape)],
        out_specs=[o[1] for o in outs],
        out_shape=[o[0] for o in outs],
        scratch_shapes=[pltpu.VMEM((FOX_HEADS, LANES), F32)],
        compiler_params=_params(("arbitrary",)),
        name="proj",
    )(h, g, wm, wft, wga, wa2, bf, ba, u)


def _fox_kernel(tq, q_ref, k_ref, v_ref, c_ref, o_ref):
    i = pl.program_id(2)
    q2 = q_ref[...]
    first = lax.broadcasted_iota(jnp.int32, (1, LANES), 1) < FOX_DIM
    zero = jnp.zeros_like(q2)
    qs = (jnp.where(first, q2, zero), jnp.where(first, zero, q2))
    causal = (lax.broadcasted_iota(jnp.int32, (tq, tq), 0)
              >= lax.broadcasted_iota(jnp.int32, (tq, tq), 1))

    def block(j, carry, masked):
        off = pl.multiple_of(j * tq, tq)
        kj = k_ref[pl.ds(off, tq), :]
        vj = v_ref[pl.ds(off, tq), :]
        out = []
        for h in range(2):
            m, l, acc = carry[h]
            s = _dot_nt(qs[h], kj) - c_ref[h:h + 1, pl.ds(off, tq)]
            if masked:
                s = jnp.where(causal, s, NEG)
            m_new = jnp.maximum(m, jnp.max(s, axis=1, keepdims=True))
            p = jnp.exp2(s - m_new)
            alpha = jnp.exp2(m - m_new)
            l = alpha * l + jnp.sum(p, axis=1, keepdims=True)
            acc = alpha * acc + _dot(p.astype(BF16), vj)
            out.append((m_new, l, acc))
        return tuple(out)

    init = tuple((jnp.full((tq, 1), NEG, F32), jnp.zeros((tq, 1), F32), jnp.zeros((tq, LANES), F32))
                 for _ in range(2))
    carry = lax.fori_loop(0, i, lambda j, c: block(j, c, False), init)
    carry = block(i, carry, True)
    o0 = carry[0][2] / carry[0][1]
    o1 = carry[1][2] / carry[1][1]
    o_ref[...] = jnp.where(first, o0, o1).astype(BF16)


def _fox_call(qb, kb, vb, ct, nb, seq, tq):
    n = qb.shape[0]
    nq = seq // tq
    pairs = FOX_HEADS // 2
    k3 = kb.reshape(nb, seq, FOX_WIDTH)
    v3 = vb.reshape(nb, seq, FOX_WIDTH)
    c4 = ct.reshape(nb, pairs, 2, seq)
    qspec = pl.BlockSpec((tq, LANES), lambda b, p, i: (b * nq + i, p))
    kvspec = pl.BlockSpec((None, seq, LANES), lambda b, p, i: (b, 0, p))
    cspec = pl.BlockSpec((None, None, 2, seq), lambda b, p, i: (b, p, 0, 0))
    return pl.pallas_call(
        functools.partial(_fox_kernel, tq),
        grid=(nb, pairs, nq),
        in_specs=[qspec, kvspec, kvspec, cspec],
        out_specs=qspec,
        out_shape=jax.ShapeDtypeStruct((n, FOX_WIDTH), BF16),
        compiler_params=_params(("arbitrary", "arbitrary", "arbitrary")),
        name="fox",
    )(qb, k3, v3, c4)


def _gla_kernel(chunk, nchunk, q_ref, k_ref, la_ref, v_ref, g_ref, st0_ref, gn_ref, tri_ref,
                go_ref, sto_ref, st_ref):
    t = pl.program_id(1)

    @pl.when(t == 0)
    def _():
        st_ref[...] = st0_ref[...]

    first = lax.broadcasted_iota(jnp.int32, (1, LANES), 1) < GLA_KEY
    tril = (lax.broadcasted_iota(jnp.int32, (chunk, chunk), 0)
            >= lax.broadcasted_iota(jnp.int32, (chunk, chunk), 1))
    blockdiag = ((lax.broadcasted_iota(jnp.int32, (2 * GLA_VAL, LANES), 0) < GLA_VAL)
                 == (lax.broadcasted_iota(jnp.int32, (2 * GLA_VAL, LANES), 1) < GLA_KEY))
    gn = gn_ref[...]
    tri = tri_ref[...]
    for ci in range(nchunk):
        rs = slice(ci * chunk, (ci + 1) * chunk)
        for p in range(2):
            ls = slice(p * LANES, (p + 1) * LANES)
            hi, mid, lo = _split3(la_ref[rs, ls])
            b = _dot(tri, hi) + _dot(tri, mid) + _dot(tri, lo)
            bl = b[chunk - 1:chunk, :]
            qd = q_ref[rs, ls] * jnp.exp(b)
            kk = k_ref[rs, ls]
            kin = (kk * jnp.exp(-b)).astype(BF16)
            kout = (kk * jnp.exp(bl - b)).astype(BF16)
            st = st_ref[p]
            o_inter = _dot_nt(qd.astype(BF16), st.astype(BF16))
            for hh in range(2):
                qm = jnp.where(first if hh == 0 else jnp.logical_not(first), qd, 0.0).astype(BF16)
                a = jnp.where(tril, _dot_nt(qm, kin), 0.0)
                vs = slice((2 * p + hh) * GLA_VAL, (2 * p + hh + 1) * GLA_VAL)
                o = (o_inter[:, hh * GLA_VAL:(hh + 1) * GLA_VAL]
                     + _dot(a.astype(BF16), v_ref[rs, vs].astype(BF16)))
                go_ref[rs, vs] = (_rms(o, gn) * _silu(g_ref[rs, vs])).astype(BF16)
            vp = v_ref[rs, 2 * p * GLA_VAL:(2 * p + 2) * GLA_VAL].astype(BF16)
            ut = _dot_tn(vp, kout)
            st_ref[p] = st * jnp.exp(bl) + jnp.where(blockdiag, ut, 0.0)
    sto_ref[...] = st_ref[...]


def _gla_call(gq, gk, la, gv, gg, st0, gn, tri, nb, seq, chunk, nchunk):
    n = gq.shape[0]
    tg = chunk * nchunk
    spt = seq // tg
    row = lambda w: pl.BlockSpec((tg, w), lambda b, t: (b * spt + t, 0))
    stspec = pl.BlockSpec((None, 2, 2 * GLA_VAL, LANES), lambda b, t: (b, 0, 0, 0))
    return pl.pallas_call(
        functools.partial(_gla_kernel, chunk, nchunk),
        grid=(nb, spt),
        in_specs=[row(GLA_KW), row(GLA_KW), row(GLA_KW), row(GLA_VW), row(GLA_VW), stspec,
                  _full(gn.shape), _full(tri.shape)],
        out_specs=[row(GLA_VW), stspec],
        out_shape=[jax.ShapeDtypeStruct((n, GLA_VW), BF16),
                   jax.ShapeDtypeStruct(st0.shape, F32)],
        scratch_shapes=[pltpu.VMEM((2, 2 * GLA_VAL, LANES), F32)],
        compiler_params=_params(("arbitrary", "arbitrary")),
        name="gla",
    )(gq, gk, la, gv, gg, st0, gn, tri)


_ROWS = FOX_HEADS * 8
_KEYS = PAGE * FOX_HEADS


def _fox_dec_kernel(npp, pt_ref, qbd_ref, *refs):
    del pt_ref
    k_refs, v_refs, f_refs = refs[0:npp], refs[npp:2 * npp], refs[2 * npp:3 * npp]
    knew_ref, vnew_ref, cn_ref, sel_ref, w3_ref, mpast_ref, mnew_ref = refs[3 * npp:3 * npp + 7]
    o_ref = refs[3 * npp + 7]
    m_ref, l_ref, acc_ref, carry_ref = refs[3 * npp + 8:]
    j = pl.program_id(1)

    @pl.when(j == 0)
    def _():
        m_ref[...] = jnp.full(m_ref.shape, NEG, F32)
        l_ref[...] = jnp.zeros_like(l_ref)
        acc_ref[...] = jnp.zeros_like(acc_ref)
        carry_ref[...] = jnp.zeros_like(carry_ref)

    qbd = qbd_ref[...]

    def update(s, vals):
        m = m_ref[...]
        m_new = jnp.maximum(m, jnp.max(s, axis=1, keepdims=True))
        p = jnp.exp2(s - m_new)
        alpha = jnp.exp2(m - m_new)
        l_ref[...] = alpha * l_ref[...] + jnp.sum(p, axis=1, keepdims=True)
        acc_ref[...] = alpha * acc_ref[...] + _dot(p.astype(BF16), vals)
        m_ref[...] = m_new

    sel = sel_ref[...]
    for r in range(npp):
        kp = k_refs[r][...].reshape(_KEYS, FOX_DIM).astype(BF16)
        vp = v_refs[r][...].reshape(_KEYS, FOX_DIM).astype(BF16)
        fh, fm, fl = _split3(f_refs[r][...] * LOG2E)
        ft = _dot_nt(sel, fh) + _dot_nt(sel, fm) + _dot_nt(sel, fl)
        th, tm_, tl = _split3(ft)
        bt = _dot(jnp.concatenate([th, tm_, tl], axis=1), w3_ref[...])
        carry = carry_ref[...]
        s = _dot_nt(qbd, kp) + (bt[:, :_KEYS] + carry[:, 0:1]) + mpast_ref[...]
        carry_ref[...] = carry + bt[:, _KEYS:]
        update(s, vp)

    @pl.when(j == pl.num_programs(1) - 1)
    def _():
        s = _dot_nt(qbd, knew_ref[...]) + cn_ref[...] + mnew_ref[...]
        update(s, vnew_ref[...])
        o_ref[...] = acc_ref[...] / l_ref[...]


def _fox_dec_call(page_table, qbd, cache_k, cache_v, cache_f, knew, vnew, cnrow, layer):
    db, n_pages = page_table.shape
    npp = DEC_PAGES_PER_STEP
    steps = n_pages // npp
    sel = np.zeros((_ROWS, FOX_HEADS), np.float32)
    sel[np.arange(_ROWS), np.arange(_ROWS) // 8] = 1.0
    pos = np.arange(PAGE)
    suffix = (pos[:, None] > pos[None, :]).astype(np.float32)
    w = np.concatenate([np.repeat(suffix, FOX_HEADS, axis=1), np.ones((PAGE, LANES), np.float32)], 1)
    w3 = np.concatenate([w, w, w], axis=0)
    row_head = np.arange(_ROWS) // 8
    row_tok = np.arange(_ROWS) % 8
    mpast = np.where(row_head[:, None] == (np.arange(_KEYS) % FOX_HEADS)[None, :], 0.0, NEG)
    col = np.arange(_ROWS)
    mnew = np.where((row_head[:, None] == (col % FOX_HEADS)[None, :])
                    & ((col // FOX_HEADS)[None, :] <= row_tok[:, None]), 0.0, NEG)

    def page_spec(r, tail):
        blk = (None, None, PAGE, FOX_HEADS) + tail
        zeros = (0,) * (1 + len(tail))
        return pl.BlockSpec(
            blk, lambda b, j, pt: (layer, pt[b, n_pages - 1 - (j * npp + r)], 0) + zeros)

    seq_spec = lambda shape: pl.BlockSpec((None,) + shape, lambda b, j, pt: (b,) + (0,) * len(shape))
    const = lambda a: pl.BlockSpec(a.shape, lambda b, j, pt: (0,) * a.ndim)
    consts = [jnp.asarray(sel, BF16), jnp.asarray(w3, BF16), jnp.asarray(mpast, F32),
              jnp.asarray(mnew, F32)]
    in_specs = ([seq_spec((_ROWS, FOX_DIM))]
                + [page_spec(r, (FOX_DIM,)) for r in range(npp)]
                + [page_spec(r, (FOX_DIM,)) for r in range(npp)]
                + [page_spec(r, ()) for r in range(npp)]
                + [seq_spec((_ROWS, FOX_DIM)), seq_spec((_ROWS, FOX_DIM)), seq_spec((1, _ROWS))]
                + [const(a) for a in consts])
    grid_spec = pltpu.PrefetchScalarGridSpec(
        num_scalar_prefetch=1,
        grid=(db, steps),
        in_specs=in_specs,
        out_specs=seq_spec((_ROWS, FOX_DIM)),
        scratch_shapes=[pltpu.VMEM((_ROWS, 1), F32), pltpu.VMEM((_ROWS, 1), F32),
                        pltpu.VMEM((_ROWS, FOX_DIM), F32), pltpu.VMEM((_ROWS, LANES), F32)],
    )
    return pl.pallas_call(
        functools.partial(_fox_dec_kernel, npp),
        grid_spec=grid_spec,
        out_shape=jax.ShapeDtypeStruct((db, _ROWS, FOX_DIM), F32),
        compiler_params=_params(("arbitrary", "arbitrary")),
        name="fox_dec",
    )(page_table, qbd, *([cache_k] * npp), *([cache_v] * npp), *([cache_f] * npp),
      knew, vnew, cnrow, *consts)


def _prep_ffn(w_gu, w_down):
    pad = FFN_PAD - FFN_DIM
    gate = jnp.pad(w_gu[:, :FFN_DIM], ((0, 0), (0, pad)))
    up = jnp.pad(w_gu[:, FFN_DIM:], ((0, 0), (0, pad)))
    wgu = jnp.concatenate([gate, up], axis=1).astype(BF16)
    wd = jnp.pad(w_down, ((0, pad), (0, 0))).astype(BF16)
    return wgu, wd


def _prep_proj(w_in, w_gate_up):
    o = np.cumsum([0, FOX_WIDTH, FOX_WIDTH, FOX_WIDTH, FOX_HEADS, GLA_KW, GLA_KW, GLA_VW, GLA_VW,
                   GATE_RANK])
    fq, fk, fv, ff, gq, gk, gv, gg, ga = [w_in[:, o[i]:o[i + 1]] for i in range(9)]
    wm = jnp.concatenate([fq * (FOX_DIM ** -0.5 * LOG2E), fk, fv, gq * (GLA_KEY ** -0.5), gk, gv, gg],
                         axis=1).astype(BF16)
    wft = ff.T.astype(BF16)
    wga = jnp.pad(ga, ((0, 0), (0, LANES - GATE_RANK))).astype(BF16)
    wa2 = jnp.pad(w_gate_up, ((0, LANES - GATE_RANK), (0, 0))).astype(BF16)
    return wm, wft, wga, wa2


def _state_to_pairs(s):
    b = s.shape[0]
    st = jnp.swapaxes(s, -1, -2).reshape(b, GLA_HEADS // 2, 2, GLA_VAL, GLA_KEY)
    eye = jnp.eye(2, dtype=s.dtype)
    st = st[:, :, :, :, None, :] * eye[None, None, :, None, :, None]
    return st.reshape(b, GLA_HEADS // 2, 2 * GLA_VAL, 2 * GLA_KEY)


def _pairs_to_state(st):
    b = st.shape[0]
    r = st.reshape(b, GLA_HEADS // 2, 2, GLA_VAL, 2, GLA_KEY)
    d = jnp.stack([r[:, :, 0, :, 0, :], r[:, :, 1, :, 1, :]], axis=2)
    return jnp.swapaxes(d.reshape(b, GLA_HEADS, GLA_VAL, GLA_KEY), -1, -2)


def _pick(n, pref):
    t = min(n, pref)
    assert n % t == 0, (n, t)
    return t


def kernel(x_prompt, x_sample, cache_k, cache_v, cache_logf, state_gla, page_table, ffn1_norm_pre, ffn1_w_gu, ffn1_w_down, ffn1_norm_post, mix_norm_pre, w_in, b_forget, w_gate_up, b_gate, gla_norm, w_out, mix_norm_post, ffn2_norm_pre, ffn2_w_gu, ffn2_w_down, ffn2_norm_post):
    nb, seq, _ = x_prompt.shape
    db, dseq, _ = x_sample.shape
    depth = ffn1_w_gu.shape[0]
    assert dseq == 8 and page_table.shape[1] % DEC_PAGES_PER_STEP == 0
    n_p, n_s = nb * seq, db * dseq
    tm_p, tm_s = _pick(n_p, 256), _pick(n_s, 256)
    tq = _pick(seq, 512)
    nchunk_p = _pick(seq // GLA_CHUNK, 4)

    hp = x_prompt.reshape(n_p, D_MODEL)
    hs = x_sample.reshape(n_s, D_MODEL)
    u_p = jnp.asarray(np.triu(np.ones((tm_p, tm_p), np.float32)), BF16)
    blk = np.arange(tm_s) // dseq
    u_s = jnp.asarray(np.triu(np.ones((tm_s, tm_s), np.float32)) * (blk[:, None] == blk[None, :]), BF16)
    tri_p = jnp.asarray(np.tril(np.ones((GLA_CHUNK, GLA_CHUNK), np.float32)), BF16)
    tri_s = jnp.asarray(np.tril(np.ones((dseq, dseq), np.float32)), BF16)
    row2 = lambda a: a.reshape(1, -1)

    outs = {k: [] for k in ("kp", "vp", "fp", "sp", "ks", "vs", "fs", "ss")}
    for l in range(depth):
        wgu1, wd1 = _prep_ffn(ffn1_w_gu[l], ffn1_w_down[l])
        wgu2, wd2 = _prep_ffn(ffn2_w_gu[l], ffn2_w_down[l])
        wm, wft, wga, wa2 = _prep_proj(w_in[l], w_gate_up[l])
        wo = w_out[l].astype(BF16)
        g1pre, g1post = row2(ffn1_norm_pre[l]), row2(ffn1_norm_post[l])
        g2pre, g2post = row2(ffn2_norm_pre[l]), row2(ffn2_norm_post[l])
        gmpre, gmpost = row2(mix_norm_pre[l]), row2(mix_norm_post[l])
        bf = b_forget[l].reshape(FOX_HEADS, 1)
        ba = row2(b_gate[l])
        gn = row2(gla_norm[l])

        hp = _ffn_call(hp, g1pre, wgu1, wd1, g1post, tm_p)
        (qb, kb, vb, fk, fv, lft, ct, gq, gk, la, gv, gg) = _proj_call(
            hp, gmpre, wm, wft, wga, wa2, bf, ba, u_p, nb, seq, tm_p)
        fo = _fox_call(qb, kb, vb, ct, nb, seq, tq)
        st0 = jnp.zeros((nb, GLA_HEADS // 2, 2 * GLA_VAL, LANES), F32)
        go, stp = _gla_call(gq, gk, la, gv, gg, st0, gn, tri_p, nb, seq, GLA_CHUNK, nchunk_p)
        hp = _merge_ffn_call(fo, go, hp, wo, gmpost, g2pre, wgu2, wd2, g2post, tm_p)
        outs["kp"].append(fk.reshape(nb, seq, FOX_HEADS, FOX_DIM))
        outs["vp"].append(fv.reshape(nb, seq, FOX_HEADS, FOX_DIM))
        outs["fp"].append(jnp.swapaxes(lft, 1, 2))
        outs["sp"].append(_pairs_to_state(stp))

        hs = _ffn_call(hs, g1pre, wgu1, wd1, g1post, tm_s)
        (qb, kb, vb, fk, fv, lft, ct, gq, gk, la, gv, gg) = _proj_call(
            hs, gmpre, wm, wft, wga, wa2, bf, ba, u_s, 1, n_s, tm_s)
        qbd = jnp.swapaxes(qb.reshape(db, dseq, FOX_HEADS, FOX_DIM), 1, 2).reshape(db, _ROWS, FOX_DIM)
        knew = kb.reshape(db, _ROWS, FOX_DIM)
        vnew = vb.reshape(db, _ROWS, FOX_DIM)
        cn = jnp.transpose(ct[0].reshape(FOX_HEADS, db, dseq), (1, 2, 0))
        fo_s = _fox_dec_call(page_table, qbd, cache_k, cache_v, cache_logf, knew, vnew,
                             -cn.reshape(db, 1, _ROWS), l)
        fo_s = jnp.swapaxes(fo_s.reshape(db, FOX_HEADS, dseq, FOX_DIM), 1, 2).reshape(n_s, FOX_WIDTH)
        go_s, sts = _gla_call(gq, gk, la, gv, gg, _state_to_pairs(state_gla[l]), gn, tri_s,
                              db, dseq, dseq, 1)
        hs = _merge_ffn_call(fo_s.astype(BF16), go_s, hs, wo, gmpost, g2pre, wgu2, wd2, g2post, tm_s)
        outs["ks"].append(fk.reshape(db, dseq, FOX_HEADS, FOX_DIM))
        outs["vs"].append(fv.reshape(db, dseq, FOX_HEADS, FOX_DIM))
        lfs = jnp.transpose(lft[0].reshape(FOX_HEADS, db, dseq), (1, 2, 0))
        outs["fs"].append(lfs)
        outs["ss"].append(_pairs_to_state(sts))

    stack = lambda k, dt: jnp.stack(outs[k], 0).astype(dt)
    return (hp.reshape(nb, seq, D_MODEL), hs.reshape(db, dseq, D_MODEL),
            stack("kp", cache_k.dtype), stack("vp", cache_v.dtype), stack("fp", cache_logf.dtype),
            stack("sp", state_gla.dtype),
            stack("ks", cache_k.dtype), stack("vs", cache_v.dtype), stack("fs", cache_logf.dtype),
            stack("ss", state_gla.dtype))
```

```python
import functools

import numpy as np
import jax
import jax.numpy as jnp
from jax import lax
from jax.experimental import pallas as pl
from jax.experimental.pallas import tpu as pltpu

BF16 = jnp.bfloat16
F32 = jnp.float32

D_MODEL = 1024
FOX_HEADS = 8
FOX_DIM = 64
FOX_WIDTH = FOX_HEADS * FOX_DIM
GLA_HEADS = 4
GLA_KEY = 64
GLA_VAL = 128
GLA_KW = GLA_HEADS * GLA_KEY
GLA_VW = GLA_HEADS * GLA_VAL
GATE_RANK = 16
GATE_TAU = 16.0
GLA_CHUNK = 64
FFN_DIM = 2752
LANES = 128
FFN_PAD = 2816
FFN_CHUNK = 256
NORM_EPS = 1e-6
PAGE = 128
LOG2E = 1.4426950408889634
NEG = -1e30
VMEM_LIMIT_BYTES = 56 * 1024 * 1024
DEC_PAGES_PER_STEP = 8


def _dot(a, b):
    return jnp.dot(a, b, preferred_element_type=F32)


def _dot_nt(a, b):
    return lax.dot_general(a, b, (((1,), (1,)), ((), ())), preferred_element_type=F32)


def _dot_tn(a, b):
    return lax.dot_general(a, b, (((0,), (0,)), ((), ())), preferred_element_type=F32)


def _split3(x):
    hi = x.astype(BF16)
    r = x - hi.astype(F32)
    mid = r.astype(BF16)
    r = r - mid.astype(F32)
    return hi, mid, r.astype(BF16)


def _rms(x, g):
    ms = jnp.mean(x * x, axis=-1, keepdims=True)
    return x * lax.rsqrt(ms + NORM_EPS) * g


def _silu(x):
    return x * jax.nn.sigmoid(x)


def _log_sigmoid(x):
    return jnp.minimum(x, 0.0) - jnp.log1p(jnp.exp(-jnp.abs(x)))


def _params(sem):
    return pltpu.CompilerParams(dimension_semantics=sem, vmem_limit_bytes=VMEM_LIMIT_BYTES)


def _full(shape):
    nd = len(shape)
    return pl.BlockSpec(shape, lambda *_: (0,) * nd)


def _ffn_body(x, gpre, wgu_ref, wd_ref, gpost):
    xb = _rms(x, gpre).astype(BF16)
    acc = jnp.zeros((x.shape[0], D_MODEL), F32)
    for c in range(FFN_PAD // FFN_CHUNK):
        lo, hi = c * FFN_CHUNK, (c + 1) * FFN_CHUNK
        gate = _dot(xb, wgu_ref[:, lo:hi])
        up = _dot(xb, wgu_ref[:, FFN_PAD + lo:FFN_PAD + hi])
        act = (_silu(gate) * up).astype(BF16)
        acc = acc + _dot(act, wd_ref[lo:hi, :])
    return x + 0.5 * _rms(acc, gpost)


def _ffn_kernel(x_ref, gpre_ref, wgu_ref, wd_ref, gpost_ref, o_ref):
    o_ref[...] = _ffn_body(x_ref[...], gpre_ref[...], wgu_ref, wd_ref, gpost_ref[...])


def _merge_ffn_kernel(fo_ref, go_ref, h_ref, wo_ref, gmix_ref, gpre_ref, wgu_ref, wd_ref, gpost_ref,
                      o_ref):
    mix = _dot(fo_ref[...], wo_ref[0:FOX_WIDTH, :]) + _dot(go_ref[...], wo_ref[FOX_WIDTH:, :])
    h2 = h_ref[...] + _rms(mix, gmix_ref[...])
    o_ref[...] = _ffn_body(h2, gpre_ref[...], wgu_ref, wd_ref, gpost_ref[...])


def _ffn_call(x, gpre, wgu, wd, gpost, tm):
    n = x.shape[0]
    row = pl.BlockSpec((tm, D_MODEL), lambda i: (i, 0))
    return pl.pallas_call(
        _ffn_kernel,
        grid=(n // tm,),
        in_specs=[row, _full(gpre.shape), _full(wgu.shape), _full(wd.shape), _full(gpost.shape)],
        out_specs=row,
        out_shape=jax.ShapeDtypeStruct((n, D_MODEL), F32),
        compiler_params=_params(("arbitrary",)),
        name="ffn",
    )(x, gpre, wgu, wd, gpost)


def _merge_ffn_call(fo, go, h, wo, gmix, gpre, wgu, wd, gpost, tm):
    n = h.shape[0]
    row = pl.BlockSpec((tm, D_MODEL), lambda i: (i, 0))
    half = pl.BlockSpec((tm, FOX_WIDTH), lambda i: (i, 0))
    return pl.pallas_call(
        _merge_ffn_kernel,
        grid=(n // tm,),
        in_specs=[half, half, row, _full(wo.shape), _full(gmix.shape), _full(gpre.shape),
                  _full(wgu.shape), _full(wd.shape), _full(gpost.shape)],
        out_specs=row,
        out_shape=jax.ShapeDtypeStruct((n, D_MODEL), F32),
        compiler_params=_params(("arbitrary",)),
        name="merge_ffn",
    )(fo, go, h, wo, gmix, gpre, wgu, wd, gpost)


_GQ0 = FOX_WIDTH
_GK0 = _GQ0 + GLA_KW
_GV0 = _GK0 + GLA_KW
_GG0 = _GV0 + GLA_VW
_MAIN_W = _GG0 + GLA_VW


def _proj_kernel(steps_per_seq, h_ref, g_ref, wm_ref, wkvt_ref, wft_ref, wga_ref, wa2_ref, bf_ref,
                 ba_ref, u_ref, qb_ref, ktb_ref, vtb_ref, kt_ref, vt_ref, lft_ref, ct_ref, gq_ref,
                 gk_ref, la_ref, gv_ref, gg_ref, carry_ref):
    i = pl.program_id(0)
    ub = _rms(h_ref[...], g_ref[...]).astype(BF16)
    z = _dot(ub, wm_ref[...])
    qb_ref[...] = z[:, 0:_GQ0].astype(BF16)
    gq_ref[...] = z[:, _GQ0:_GK0]
    gk_ref[...] = z[:, _GK0:_GV0]
    gv_ref[...] = z[:, _GV0:_GG0]
    gg_ref[...] = z[:, _GG0:_MAIN_W]

    zt = _dot_nt(wkvt_ref[...], ub)
    kt = zt[0:FOX_WIDTH]
    vt = zt[FOX_WIDTH:]
    kt_ref[...] = kt
    vt_ref[...] = vt
    ktb_ref[...] = kt.astype(BF16)
    vtb_ref[...] = vt.astype(BF16)

    lf = _log_sigmoid(_dot_nt(wft_ref[...], ub) + bf_ref[...])
    lft_ref[...] = lf
    hi, mid, lo = _split3(lf)
    u = u_ref[...]
    loc = _dot(hi, u) + _dot(mid, u) + _dot(lo, u)

    @pl.when(i % steps_per_seq == 0)
    def _():
        carry_ref[...] = jnp.zeros_like(carry_ref)

    c = loc + carry_ref[:, 0:1]
    tm = c.shape[1]
    carry_ref[...] = jnp.broadcast_to(c[:, tm - 1:tm], carry_ref.shape)
    ct_ref[...] = c * LOG2E

    ga = _dot(ub, wga_ref[...]).astype(BF16)
    pre = _dot(ga, wa2_ref[...]) + ba_ref[...]
    la_ref[...] = _log_sigmoid(pre) * (1.0 / GATE_TAU)


def _proj_call(h, g, wm, wkvt, wft, wga, wa2, bf, ba, u, nb, seq, tm):
    n = h.shape[0]
    spt = seq // tm
    row = lambda w: pl.BlockSpec((tm, w), lambda i: (i, 0))
    tok = lambda r: pl.BlockSpec((None, r, tm), lambda i: (i // spt, 0, i % spt))
    sds = jax.ShapeDtypeStruct
    outs = [
        (sds((n, FOX_WIDTH), BF16), row(FOX_WIDTH)),
        (sds((nb, FOX_WIDTH, seq), BF16), tok(FOX_WIDTH)),
        (sds((nb, FOX_WIDTH, seq), BF16), tok(FOX_WIDTH)),
        (sds((nb, FOX_WIDTH, seq), F32), tok(FOX_WIDTH)),
        (sds((nb, FOX_WIDTH, seq), F32), tok(FOX_WIDTH)),
        (sds((nb, FOX_HEADS, seq), F32), tok(FOX_HEADS)),
        (sds((nb, FOX_HEADS, seq), F32), tok(FOX_HEADS)),
        (sds((n, GLA_KW), F32), row(GLA_KW)),
        (sds((n, GLA_KW), F32), row(GLA_KW)),
        (sds((n, GLA_KW), F32), row(GLA_KW)),
        (sds((n, GLA_VW), F32), row(GLA_VW)),
        (sds((n, GLA_VW), F32), row(GLA_VW)),
    ]
    return pl.pallas_call(
        functools.partial(_proj_kernel, spt),
        grid=(n // tm,),
        in_specs=[row(D_MODEL), _full(g.shape), _full(wm.shape), _full(wkvt.shape), _full(wft.shape),
                  _full(wga.shape), _full(wa2.shape), _full(bf.shape), _full(ba.shape),
                  _full(u.shape)],
        out_specs=[o[1] for o in outs],
        out_shape=[o[0] for o in outs],
        scratch_shapes=[pltpu.VMEM((FOX_HEADS, LANES), F32)],
        compiler_params=_params(("arbitrary",)),
        name="proj",
    )(h, g, wm, wkvt, wft, wga, wa2, bf, ba, u)


FOX_STRIP = 16


def _fox_kernel(tq, q_ref, kt_ref, vt_ref, c_ref, o_ref, s_ref, p_ref, m_ref, l_ref, a_ref,
                acc_ref):
    i = pl.program_id(2)
    nchunk = tq // LANES
    q2 = q_ref[...]
    first = lax.broadcasted_iota(jnp.int32, (1, LANES), 1) < FOX_DIM
    zero = jnp.zeros_like(q2)
    qs = (jnp.where(first, q2, zero), jnp.where(first, zero, q2))
    m_ref[...] = jnp.full(m_ref.shape, NEG, F32)
    l_ref[...] = jnp.zeros_like(l_ref)
    acc_ref[...] = jnp.zeros_like(acc_ref)
    row_iota = lax.broadcasted_iota(jnp.int32, (FOX_STRIP, LANES), 0)
    col_iota = lax.broadcasted_iota(jnp.int32, (FOX_STRIP, LANES), 1)

    def block(j, masked):
        off = pl.multiple_of(j * tq, tq)
        kt = kt_ref[:, pl.ds(off, tq)]
        vt = vt_ref[:, pl.ds(off, tq)]
        for h in range(2):
            s_ref[h] = _dot(qs[h], kt)
            cb = [jnp.broadcast_to(c_ref[h:h + 1, pl.ds(pl.multiple_of(off + c * LANES, LANES), LANES)],
                                   (FOX_STRIP, LANES)) for c in range(nchunk)]
            for r in range(tq // FOX_STRIP):
                r0 = r * FOX_STRIP
                rows = slice(r0, r0 + FOX_STRIP)
                live = [c for c in range(nchunk) if not masked or c * LANES <= r0 + FOX_STRIP - 1]
                ch = []
                for c in live:
                    x = s_ref[h, rows, c * LANES:(c + 1) * LANES] - cb[c]
                    if masked and (c + 1) * LANES - 1 > r0:
                        x = jnp.where(col_iota + c * LANES <= row_iota + r0, x, NEG)
                    ch.append(x)
                mx = functools.reduce(jnp.maximum, ch)
                m_old = m_ref[h, rows, :]
                m_new = jnp.maximum(m_old, jnp.max(mx, axis=1, keepdims=True))
                alpha = jnp.exp2(m_old - m_new)
                ps = [jnp.exp2(x - m_new) for x in ch]
                l_ref[h, rows, :] = alpha * l_ref[h, rows, :] + functools.reduce(jnp.add, ps)
                a_ref[h, rows, :] = alpha
                m_ref[h, rows, :] = m_new
                for c in range(nchunk):
                    pc = ps[live.index(c)].astype(BF16) if c in live else jnp.zeros((FOX_STRIP, LANES), BF16)
                    p_ref[h, rows, c * LANES:(c + 1) * LANES] = pc
            acc_ref[h] = a_ref[h] * acc_ref[h] + _dot_nt(p_ref[h], vt)

    def body(j, carry):
        block(j, False)
        return carry

    lax.fori_loop(0, i, body, 0)
    block(i, True)
    o0 = acc_ref[0] / jnp.sum(l_ref[0], axis=1, keepdims=True)
    o1 = acc_ref[1] / jnp.sum(l_ref[1], axis=1, keepdims=True)
    o_ref[...] = jnp.where(first, o0, o1).astype(BF16)


def _fox_call(qb, ktb, vtb, ct, nb, seq, tq):
    n = qb.shape[0]
    nq = seq // tq
    pairs = FOX_HEADS // 2
    c4 = ct.reshape(nb, pairs, 2, seq)
    qspec = pl.BlockSpec((tq, LANES), lambda b, p, i: (b * nq + i, p))
    kvspec = pl.BlockSpec((None, LANES, seq), lambda b, p, i: (b, p, 0))
    cspec = pl.BlockSpec((None, None, 2, seq), lambda b, p, i: (b, p, 0, 0))
    state = pltpu.VMEM((2, tq, LANES), F32)
    return pl.pallas_call(
        functools.partial(_fox_kernel, tq),
        grid=(nb, pairs, nq),
        in_specs=[qspec, kvspec, kvspec, cspec],
        out_specs=qspec,
        out_shape=jax.ShapeDtypeStruct((n, FOX_WIDTH), BF16),
        scratch_shapes=[pltpu.VMEM((2, tq, tq), F32), pltpu.VMEM((2, tq, tq), BF16),
                        state, state, state, state],
        compiler_params=_params(("arbitrary", "arbitrary", "arbitrary")),
        name="fox",
    )(qb, ktb, vtb, c4)


def _gla_kernel(chunk, nchunk, q_ref, k_ref, la_ref, v_ref, g_ref, st0_ref, gn_ref, tri_ref,
                go_ref, sto_ref, st_ref):
    t = pl.program_id(1)

    @pl.when(t == 0)
    def _():
        st_ref[...] = st0_ref[...]

    first = lax.broadcasted_iota(jnp.int32, (1, LANES), 1) < GLA_KEY
    tril = (lax.broadcasted_iota(jnp.int32, (chunk, chunk), 0)
            >= lax.broadcasted_iota(jnp.int32, (chunk, chunk), 1))
    blockdiag = ((lax.broadcasted_iota(jnp.int32, (2 * GLA_VAL, LANES), 0) < GLA_VAL)
                 == (lax.broadcasted_iota(jnp.int32, (2 * GLA_VAL, LANES), 1) < GLA_KEY))
    gn = gn_ref[...]
    tri = tri_ref[...]
    for ci in range(nchunk):
        rs = slice(ci * chunk, (ci + 1) * chunk)
        for p in range(2):
            ls = slice(p * LANES, (p + 1) * LANES)
            hi, mid, lo = _split3(la_ref[rs, ls])
            b = _dot(tri, hi) + _dot(tri, mid) + _dot(tri, lo)
            bl = b[chunk - 1:chunk, :]
            qd = q_ref[rs, ls] * jnp.exp(b)
            kk = k_ref[rs, ls]
            kin = (kk * jnp.exp(-b)).astype(BF16)
            kout = (kk * jnp.exp(bl - b)).astype(BF16)
            st = st_ref[p]
            o_inter = _dot_nt(qd.astype(BF16), st.astype(BF16))
            for hh in range(2):
                qm = jnp.where(first if hh == 0 else jnp.logical_not(first), qd, 0.0).astype(BF16)
                a = jnp.where(tril, _dot_nt(qm, kin), 0.0)
                vs = slice((2 * p + hh) * GLA_VAL, (2 * p + hh + 1) * GLA_VAL)
                o = (o_inter[:, hh * GLA_VAL:(hh + 1) * GLA_VAL]
                     + _dot(a.astype(BF16), v_ref[rs, vs].astype(BF16)))
                go_ref[rs, vs] = (_rms(o, gn) * _silu(g_ref[rs, vs])).astype(BF16)
            vp = v_ref[rs, 2 * p * GLA_VAL:(2 * p + 2) * GLA_VAL].astype(BF16)
            ut = _dot_tn(vp, kout)
            st_ref[p] = st * jnp.exp(bl) + jnp.where(blockdiag, ut, 0.0)
    sto_ref[...] = st_ref[...]


def _gla_call(gq, gk, la, gv, gg, st0, gn, tri, nb, seq, chunk, nchunk):
    n = gq.shape[0]
    tg = chunk * nchunk
    spt = seq // tg
    row = lambda w: pl.BlockSpec((tg, w), lambda b, t: (b * spt + t, 0))
    stspec = pl.BlockSpec((None, 2, 2 * GLA_VAL, LANES), lambda b, t: (b, 0, 0, 0))
    return pl.pallas_call(
        functools.partial(_gla_kernel, chunk, nchunk),
        grid=(nb, spt),
        in_specs=[row(GLA_KW), row(GLA_KW), row(GLA_KW), row(GLA_VW), row(GLA_VW), stspec,
                  _full(gn.shape), _full(tri.shape)],
        out_specs=[row(GLA_VW), stspec],
        out_shape=[jax.ShapeDtypeStruct((n, GLA_VW), BF16),
                   jax.ShapeDtypeStruct(st0.shape, F32)],
        scratch_shapes=[pltpu.VMEM((2, 2 * GLA_VAL, LANES), F32)],
        compiler_params=_params(("arbitrary", "arbitrary")),
        name="gla",
    )(gq, gk, la, gv, gg, st0, gn, tri)


_ROWS = FOX_HEADS * 8


def _fox_dec_kernel(npp, pt_ref, q_ref, *refs):
    del pt_ref
    k_refs, v_refs, f_refs = refs[0:npp], refs[npp:2 * npp], refs[2 * npp:3 * npp]
    knew_ref, vnew_ref, bnew_ref, w3_ref = refs[3 * npp:3 * npp + 4]
    o_ref = refs[3 * npp + 4]
    m_ref, l_ref, acc_ref, carry_ref = refs[3 * npp + 5:]
    j = pl.program_id(1)

    @pl.when(j == 0)
    def _():
        m_ref[...] = jnp.full(m_ref.shape, NEG, F32)
        l_ref[...] = jnp.zeros_like(l_ref)
        acc_ref[...] = jnp.zeros_like(acc_ref)
        carry_ref[...] = jnp.zeros_like(carry_ref)

    q = q_ref[...]
    heads = [slice(h * 8, (h + 1) * 8) for h in range(FOX_HEADS)]

    def attend(kts, vts, bias):
        s = jnp.concatenate([_dot(q[heads[h]], kts[h]) for h in range(FOX_HEADS)], axis=0) + bias
        m_old = m_ref[...]
        m_new = jnp.maximum(m_old, jnp.max(s, axis=1, keepdims=True))
        alpha = jnp.exp2(m_old - m_new)
        p = jnp.exp2(s - m_new[:, 0:1])
        lsum = functools.reduce(jnp.add, [p[:, c * LANES:(c + 1) * LANES]
                                          for c in range(p.shape[1] // LANES)])
        l_ref[...] = alpha * l_ref[...] + lsum
        pb = p.astype(BF16)
        pv = jnp.concatenate([_dot_nt(pb[heads[h]], vts[h]) for h in range(FOX_HEADS)], axis=0)
        acc_ref[...] = alpha[:, 0:FOX_DIM] * acc_ref[...] + pv
        m_ref[...] = m_new

    carry = carry_ref[...]
    w3 = w3_ref[...]
    biases = []
    for r in range(npp):
        fh, fm, fl = _split3(f_refs[r][...] * LOG2E)
        bt = _dot(jnp.concatenate([fh, fm, fl], axis=1), w3)
        loc = bt[:, :PAGE] + carry
        carry = carry + bt[:, PAGE:]
        biases.append(jnp.concatenate(
            [jnp.broadcast_to(loc[h:h + 1, :], (8, PAGE)) for h in range(FOX_HEADS)], axis=0))
    carry_ref[...] = carry
    kts = [jnp.concatenate([k_refs[r][h] for r in range(npp)], axis=1).astype(BF16)
           for h in range(FOX_HEADS)]
    vts = [jnp.concatenate([v_refs[r][h] for r in range(npp)], axis=1).astype(BF16)
           for h in range(FOX_HEADS)]
    attend(kts, vts, jnp.concatenate(biases, axis=1))

    @pl.when(j == pl.num_programs(1) - 1)
    def _():
        attend([knew_ref[h] for h in range(FOX_HEADS)], [vnew_ref[h] for h in range(FOX_HEADS)],
               bnew_ref[...])
        o_ref[...] = acc_ref[...] / jnp.sum(l_ref[...], axis=1, keepdims=True)


def _fox_dec_call(page_table, q, cache_kt, cache_vt, cache_ft, knew, vnew, bnew, layer):
    db, n_pages = page_table.shape
    npp = DEC_PAGES_PER_STEP
    steps = n_pages // npp
    pos = np.arange(PAGE)
    suffix = (pos[:, None] > pos[None, :]).astype(np.float32)
    w = np.concatenate([suffix, np.ones((PAGE, LANES), np.float32)], axis=1)
    w3 = jnp.asarray(np.concatenate([w, w, w], axis=0), BF16)

    def page_spec(r, tail):
        blk = (None, None, FOX_HEADS) + tail
        zeros = (0,) * len(tail)
        return pl.BlockSpec(
            blk, lambda b, j, pt: (layer, pt[b, n_pages - 1 - (j * npp + r)], 0) + zeros)

    seq_spec = lambda shape: pl.BlockSpec((None,) + shape, lambda b, j, pt: (b,) + (0,) * len(shape))
    in_specs = ([seq_spec((_ROWS, FOX_DIM))]
                + [page_spec(r, (FOX_DIM, PAGE)) for r in range(npp)]
                + [page_spec(r, (FOX_DIM, PAGE)) for r in range(npp)]
                + [page_spec(r, (PAGE,)) for r in range(npp)]
                + [seq_spec((FOX_HEADS, FOX_DIM, LANES)), seq_spec((FOX_HEADS, FOX_DIM, LANES)),
                   seq_spec((_ROWS, LANES)),
                   pl.BlockSpec(w3.shape, lambda b, j, pt: (0, 0))])
    grid_spec = pltpu.PrefetchScalarGridSpec(
        num_scalar_prefetch=1,
        grid=(db, steps),
        in_specs=in_specs,
        out_specs=seq_spec((_ROWS, FOX_DIM)),
        scratch_shapes=[pltpu.VMEM((_ROWS, LANES), F32), pltpu.VMEM((_ROWS, LANES), F32),
                        pltpu.VMEM((_ROWS, FOX_DIM), F32), pltpu.VMEM((FOX_HEADS, LANES), F32)],
    )
    return pl.pallas_call(
        functools.partial(_fox_dec_kernel, npp),
        grid_spec=grid_spec,
        out_shape=jax.ShapeDtypeStruct((db, _ROWS, FOX_DIM), F32),
        compiler_params=_params(("arbitrary", "arbitrary")),
        name="fox_dec",
    )(page_table, q, *([cache_kt] * npp), *([cache_vt] * npp), *([cache_ft] * npp),
      knew, vnew, bnew, w3)


def _prep_ffn(w_gu, w_down):
    pad = FFN_PAD - FFN_DIM
    gate = jnp.pad(w_gu[:, :FFN_DIM], ((0, 0), (0, pad)))
    up = jnp.pad(w_gu[:, FFN_DIM:], ((0, 0), (0, pad)))
    wgu = jnp.concatenate([gate, up], axis=1).astype(BF16)
    wd = jnp.pad(w_down, ((0, pad), (0, 0))).astype(BF16)
    return wgu, wd


def _prep_proj(w_in, w_gate_up):
    o = np.cumsum([0, FOX_WIDTH, FOX_WIDTH, FOX_WIDTH, FOX_HEADS, GLA_KW, GLA_KW, GLA_VW, GLA_VW,
                   GATE_RANK])
    fq, fk, fv, ff, gq, gk, gv, gg, ga = [w_in[:, o[i]:o[i + 1]] for i in range(9)]
    wm = jnp.concatenate([fq * (FOX_DIM ** -0.5 * LOG2E), gq * (GLA_KEY ** -0.5), gk, gv, gg],
                         axis=1).astype(BF16)
    wkvt = jnp.concatenate([fk, fv], axis=1).T.astype(BF16)
    wft = ff.T.astype(BF16)
    wga = jnp.pad(ga, ((0, 0), (0, LANES - GATE_RANK))).astype(BF16)
    wa2 = jnp.pad(w_gate_up, ((0, LANES - GATE_RANK), (0, 0))).astype(BF16)
    return wm, wkvt, wft, wga, wa2


def _state_to_pairs(s):
    b = s.shape[0]
    st = jnp.swapaxes(s, -1, -2).reshape(b, GLA_HEADS // 2, 2, GLA_VAL, GLA_KEY)
    eye = jnp.eye(2, dtype=s.dtype)
    st = st[:, :, :, :, None, :] * eye[None, None, :, None, :, None]
    return st.reshape(b, GLA_HEADS // 2, 2 * GLA_VAL, 2 * GLA_KEY)


def _pairs_to_state(st):
    b = st.shape[0]
    r = st.reshape(b, GLA_HEADS // 2, 2, GLA_VAL, 2, GLA_KEY)
    d = jnp.stack([r[:, :, 0, :, 0, :], r[:, :, 1, :, 1, :]], axis=2)
    return jnp.swapaxes(d.reshape(b, GLA_HEADS, GLA_VAL, GLA_KEY), -1, -2)


def _pick(n, pref):
    t = min(n, pref)
    assert n % t == 0, (n, t)
    return t


def kernel(x_prompt, x_sample, cache_k, cache_v, cache_logf, state_gla, page_table, ffn1_norm_pre, ffn1_w_gu, ffn1_w_down, ffn1_norm_post, mix_norm_pre, w_in, b_forget, w_gate_up, b_gate, gla_norm, w_out, mix_norm_post, ffn2_norm_pre, ffn2_w_gu, ffn2_w_down, ffn2_norm_post):
    nb, seq, _ = x_prompt.shape
    db, dseq, _ = x_sample.shape
    depth = ffn1_w_gu.shape[0]
    assert dseq == 8 and page_table.shape[1] % DEC_PAGES_PER_STEP == 0
    n_p, n_s = nb * seq, db * dseq
    tm_p, tm_s = _pick(n_p, 256), _pick(n_s, 256)
    tq = _pick(seq, 512)
    nchunk_p = _pick(seq // GLA_CHUNK, 4)

    hp = x_prompt.reshape(n_p, D_MODEL)
    hs = x_sample.reshape(n_s, D_MODEL)
    u_p = jnp.asarray(np.triu(np.ones((tm_p, tm_p), np.float32)), BF16)
    blk = np.arange(tm_s) // dseq
    u_s = jnp.asarray(np.triu(np.ones((tm_s, tm_s), np.float32)) * (blk[:, None] == blk[None, :]), BF16)
    tri_p = jnp.asarray(np.tril(np.ones((GLA_CHUNK, GLA_CHUNK), np.float32)), BF16)
    tri_s = jnp.asarray(np.tril(np.ones((dseq, dseq), np.float32)), BF16)
    row2 = lambda a: a.reshape(1, -1)

    cache_kt = jnp.transpose(cache_k, (0, 1, 3, 4, 2))
    cache_vt = jnp.transpose(cache_v, (0, 1, 3, 4, 2))
    cache_ft = jnp.transpose(cache_logf, (0, 1, 3, 2))
    tok_le = jnp.asarray(np.arange(dseq)[None, :] <= np.arange(dseq)[:, None])

    def heads_major(xt):
        return jnp.transpose(xt.reshape(FOX_HEADS, FOX_DIM, db, dseq), (2, 0, 1, 3))

    outs = {k: [] for k in ("kp", "vp", "fp", "sp", "ks", "vs", "fs", "ss")}
    for l in range(depth):
        wgu1, wd1 = _prep_ffn(ffn1_w_gu[l], ffn1_w_down[l])
        wgu2, wd2 = _prep_ffn(ffn2_w_gu[l], ffn2_w_down[l])
        wm, wkvt, wft, wga, wa2 = _prep_proj(w_in[l], w_gate_up[l])
        wo = w_out[l].astype(BF16)
        g1pre, g1post = row2(ffn1_norm_pre[l]), row2(ffn1_norm_post[l])
        g2pre, g2post = row2(ffn2_norm_pre[l]), row2(ffn2_norm_post[l])
        gmpre, gmpost = row2(mix_norm_pre[l]), row2(mix_norm_post[l])
        bf = b_forget[l].reshape(FOX_HEADS, 1)
        ba = row2(b_gate[l])
        gn = row2(gla_norm[l])

        hp = _ffn_call(hp, g1pre, wgu1, wd1, g1post, tm_p)
        (qb, ktb, vtb, kt, vt, lft, ct, gq, gk, la, gv, gg) = _proj_call(
            hp, gmpre, wm, wkvt, wft, wga, wa2, bf, ba, u_p, nb, seq, tm_p)
        fo = _fox_call(qb, ktb, vtb, ct, nb, seq, tq)
        st0 = jnp.zeros((nb, GLA_HEADS // 2, 2 * GLA_VAL, LANES), F32)
        go, stp = _gla_call(gq, gk, la, gv, gg, st0, gn, tri_p, nb, seq, GLA_CHUNK, nchunk_p)
        hp = _merge_ffn_call(fo, go, hp, wo, gmpost, g2pre, wgu2, wd2, g2post, tm_p)
        to_cache = lambda xt: jnp.transpose(xt.reshape(nb, FOX_HEADS, FOX_DIM, seq), (0, 3, 1, 2))
        outs["kp"].append(to_cache(kt))
        outs["vp"].append(to_cache(vt))
        outs["fp"].append(jnp.swapaxes(lft, 1, 2))
        outs["sp"].append(_pairs_to_state(stp))

        hs = _ffn_call(hs, g1pre, wgu1, wd1, g1post, tm_s)
        (qb, ktb, vtb, kt, vt, lft, ct, gq, gk, la, gv, gg) = _proj_call(
            hs, gmpre, wm, wkvt, wft, wga, wa2, bf, ba, u_s, 1, n_s, tm_s)
        q_s = jnp.swapaxes(qb.reshape(db, dseq, FOX_HEADS, FOX_DIM), 1, 2).reshape(db, _ROWS, FOX_DIM)
        lane_pad = ((0, 0), (0, 0), (0, 0), (0, LANES - dseq))
        knew = jnp.pad(heads_major(ktb[0]), lane_pad)
        vnew = jnp.pad(heads_major(vtb[0]), lane_pad)
        cn = jnp.transpose(ct[0].reshape(FOX_HEADS, db, dseq), (1, 0, 2))
        bnew = jnp.where(tok_le[None, None], -cn[:, :, None, :], NEG)
        bnew = jnp.pad(bnew, lane_pad, constant_values=NEG).reshape(db, _ROWS, LANES)
        fo_s = _fox_dec_call(page_table, q_s, cache_kt, cache_vt, cache_ft, knew, vnew, bnew, l)
        fo_s = jnp.swapaxes(fo_s.reshape(db, FOX_HEADS, dseq, FOX_DIM), 1, 2).reshape(n_s, FOX_WIDTH)
        go_s, sts = _gla_call(gq, gk, la, gv, gg, _state_to_pairs(state_gla[l]), gn, tri_s,
                              db, dseq, dseq, 1)
        hs = _merge_ffn_call(fo_s.astype(BF16), go_s, hs, wo, gmpost, g2pre, wgu2, wd2, g2post, tm_s)
        outs["ks"].append(jnp.transpose(heads_major(kt[0]), (0, 3, 1, 2)))
        outs["vs"].append(jnp.transpose(heads_major(vt[0]), (0, 3, 1, 2)))
        lfs = jnp.transpose(lft[0].reshape(FOX_HEADS, db, dseq), (1, 2, 0))
        outs["fs"].append(lfs)
        outs["ss"].append(_pairs_to_state(sts))

    stack = lambda k, dt: jnp.stack(outs[k], 0).astype(dt)
    return (hp.reshape(nb, seq, D_MODEL), hs.reshape(db, dseq, D_MODEL),
            stack("kp", cache_k.dtype), stack("vp", cache_v.dtype), stack("fp", cache_logf.dtype),
            stack("sp", state_gla.dtype),
            stack("ks", cache_k.dtype), stack("vs", cache_v.dtype), stack("fs", cache_logf.dtype),
            stack("ss", state_gla.dtype))
```
